```python
import math
import functools
import jax
import jax.numpy as jnp
from jax import lax
import numpy as np

D_MODEL = 1024
BATCH = 2
SEQ = 8192
DEPTH = 1
DEC_BATCH = 128
DEC_SEQ = 1
PAST_LEN = 2048
PAGE_SIZE = 128

HEAD_DIM = 64
HPG = 4
GROUPS = ((128, 1), (512, 4), (2048, 16))
N_HEADS = len(GROUPS) * HPG
ATTN_W = N_HEADS * HEAD_DIM
ATTN_OUT_W = HPG * HEAD_DIM
D_CONV = D_MODEL
CONV_K = 31
D_FF = int(math.ceil(8 * D_MODEL / 3 / 256)) * 256
D_PLE = 256
N_BUCKETS = 32
MAX_DIST = 2048
BLK = 128
EPS = 1e-6
NEG = -1e30
SCALE = HEAD_DIM ** -0.5
IN_SPLITS = (ATTN_W, 2 * ATTN_W, 3 * ATTN_W, 3 * ATTN_W + 2 * D_CONV)
IN_W = 3 * ATTN_W + 2 * D_CONV + 2 * D_MODEL

kernel_name = 'hybrid_dilated_attn_conformer_conv_decoder_step'


def t5_bucket(dist):
    dist = np.asarray(dist).astype(np.int32)
    max_exact = N_BUCKETS // 2
    d = np.maximum(dist, 1).astype(np.float32)
    large = max_exact + np.floor(np.log(d / max_exact) / np.log(MAX_DIST / max_exact)
                                 * (N_BUCKETS - max_exact)).astype(np.int32)
    large = np.minimum(large, N_BUCKETS - 1)
    return np.where(dist < max_exact, dist, large).astype(np.int32)


def rms_norm(x, g):
    xf = x.astype(jnp.float32)
    y = xf * lax.rsqrt(jnp.mean(xf * xf, axis=-1, keepdims=True) + EPS)
    return (y * g.astype(jnp.float32)).astype(x.dtype)


def layer_norm(x, g, b):
    xf = x.astype(jnp.float32)
    mu = jnp.mean(xf, axis=-1, keepdims=True)
    xc = xf - mu
    y = xc * lax.rsqrt(jnp.mean(xc * xc, axis=-1, keepdims=True) + EPS)
    return (y * g.astype(jnp.float32) + b.astype(jnp.float32)).astype(x.dtype)


def band_dilated_attention(q, k, v, table_g, dil, n_back):
    B, S, H, Dh = q.shape
    L = S // dil
    nb = -(-L // BLK)
    Lp = nb * BLK

    def to_blocks(a):
        a = a.reshape(B, L, dil, H, Dh)
        a = jnp.pad(a, ((0, 0), (0, Lp - L), (0, 0), (0, 0), (0, 0)))
        return a.reshape(B, nb, BLK, dil, H, Dh)

    def with_prev(a):
        prev = jnp.pad(a[:, :-1], ((0, 0), (1, 0), (0, 0), (0, 0), (0, 0), (0, 0)))
        return jnp.concatenate([prev, a], axis=2)

    qb = to_blocks(q)
    kw = with_prev(to_blocks(k))
    vw = with_prev(to_blocks(v))
    qq = np.arange(BLK)[:, None]
    kk = np.arange(2 * BLK)[None, :]
    j = qq + BLK - kk
    band = (j >= 0) & (j <= n_back)
    valid = band[None] & ((np.arange(nb)[:, None, None] > 0) | (kk[None] >= BLK))
    bias = jnp.transpose(table_g[t5_bucket(np.clip(j, 0, n_back) * dil)], (2, 0, 1)).astype(jnp.float32)
    s = jnp.einsum('bnqrhd,bnkrhd->bnrhqk', qb, kw, preferred_element_type=jnp.float32) * SCALE + bias
    s = jnp.where(valid[None, :, None, None], s, NEG)
    lse = jax.nn.logsumexp(s, axis=-1)
    pr = jnp.exp(s - lse[..., None])
    o = jnp.einsum('bnrhqk,bnkrhd->bnqrhd', pr, vw.astype(jnp.float32))
    o = o.reshape(B, Lp, dil, H, Dh)[:, :L].reshape(B, S, H, Dh)
    lse = jnp.transpose(lse, (0, 1, 4, 2, 3)).reshape(B, Lp, dil, H)[:, :L].reshape(B, S, H)
    return o, lse


def gathered_dilated_attention(q, k_all, v_all, table_g, dil, n_back):
    T = q.shape[1]
    hist = k_all.shape[1] - T
    j = np.arange(n_back + 1)
    idx = hist + np.arange(T)[:, None] - j[None, :] * dil
    valid = idx >= 0
    idx = np.maximum(idx, 0)
    kg = k_all[:, idx]
    vg = v_all[:, idx]
    bias = jnp.transpose(table_g[t5_bucket(j * dil)]).astype(jnp.float32)
    s = jnp.einsum('bthd,btjhd->bthj', q, kg, preferred_element_type=jnp.float32) * SCALE + bias[None, None]
    s = jnp.where(valid[None, :, None, :], s, NEG)
    lse = jax.nn.logsumexp(s, axis=-1)
    pr = jnp.exp(s - lse[..., None])
    o = jnp.einsum('bthj,btjhd->bthd', pr, vg.astype(jnp.float32))
    return o, lse


def combine_groups(outs, lses):
    w = jax.nn.softmax(jnp.stack(lses, axis=0), axis=0)
    return jnp.einsum('gbth,gbthd->bthd', w, jnp.stack(outs, axis=0))


def prompt_attention(q, k, v, rel_bias):
    S = q.shape[1]
    outs, lses, bufs = [], [], []
    for g, (win, dil) in enumerate(GROUPS):
        hs = slice(g * HPG, (g + 1) * HPG)
        o, lse = band_dilated_attention(q[:, :, hs], k[:, :, hs], v[:, :, hs], rel_bias[:, hs], dil, win // dil)
        outs.append(o)
        lses.append(lse)
        keep = min(win, S)
        bufs.append(jnp.stack([k[:, S - keep:, hs], v[:, S - keep:, hs]], axis=2))
    return combine_groups(outs, lses), bufs


def sample_attention(q, k, v, rel_bias, kv_caches):
    outs, lses, bufs = [], [], []
    for g, (win, dil) in enumerate(GROUPS):
        hs = slice(g * HPG, (g + 1) * HPG)
        hist = kv_caches[g]
        Lg = hist.shape[1]
        k_all = jnp.concatenate([hist[:, :, 0], k[:, :, hs].astype(hist.dtype)], axis=1)
        v_all = jnp.concatenate([hist[:, :, 1], v[:, :, hs].astype(hist.dtype)], axis=1)
        o, lse = gathered_dilated_attention(q[:, :, hs], k_all, v_all, rel_bias[:, hs], dil, win // dil)
        outs.append(o)
        lses.append(lse)
        n = k_all.shape[1]
        bufs.append(jnp.stack([k_all[:, n - Lg:], v_all[:, n - Lg:]], axis=2))
    return combine_groups(outs, lses), bufs


def conformer_conv(u, hist, conv_w, conv_b, ln_g, ln_b, w_conv_out):
    a, gte = jnp.split(u, 2, axis=-1)
    z = a * jax.nn.sigmoid(gte)
    zc = jnp.concatenate([hist.astype(z.dtype), z], axis=1)
    y = lax.conv_general_dilated(zc, conv_w[:, None, :].astype(z.dtype), window_strides=(1,),
                                 padding='VALID', dimension_numbers=('NWC', 'WIO', 'NWC'),
                                 feature_group_count=D_CONV) + conv_b
    y = layer_norm(y, ln_g, ln_b)
    y = y * jax.nn.sigmoid(y)
    return y @ w_conv_out, zc[:, zc.shape[1] - (CONV_K - 1):]


def decoder_layer(x, p, attn_fn, conv_hist, norm_mix_pre, w_in, conv_w, conv_b, conv_ln_g, conv_ln_b,
                  w_conv_out, w_attn_out, w_out, norm_mix_post, norm_ffn_pre, w_ffn_in, w_ffn_out,
                  norm_ffn_post, ple_norm, w_ple_gate, w_ple_proj):
    B, T, _ = x.shape
    h = rms_norm(x, norm_mix_pre)
    q, k, v, u, gates = jnp.split(h @ w_in, IN_SPLITS, axis=-1)
    q = q.reshape(B, T, N_HEADS, HEAD_DIM)
    k = k.reshape(B, T, N_HEADS, HEAD_DIM)
    v = v.reshape(B, T, N_HEADS, HEAD_DIM)
    o, kv_bufs = attn_fn(q, k, v)
    a = o.reshape(B, T, ATTN_OUT_W).astype(x.dtype) @ w_attn_out
    c, conv_tail = conformer_conv(u, conv_hist, conv_w, conv_b, conv_ln_g, conv_ln_b, w_conv_out)
    g = jax.nn.sigmoid(gates).reshape(B, T, 2, D_MODEL)
    mix = (g[:, :, 0] * a + g[:, :, 1] * c) @ w_out
    x = x + rms_norm(mix, norm_mix_post)
    h2 = rms_norm(x, norm_ffn_pre)
    gg, uu = jnp.split(h2 @ w_ffn_in, 2, axis=-1)
    f = (jax.nn.silu(gg) * uu) @ w_ffn_out
    x = x + rms_norm(f, norm_ffn_post)
    x = x + jax.nn.sigmoid(rms_norm(x, ple_norm) @ w_ple_gate) * (p @ w_ple_proj)
    return x, kv_bufs, conv_tail


def setup_inputs(seed: int = 0) -> dict:
    key = jax.random.key(seed)
    ks = jax.random.split(key, 32)
    f32 = jnp.float32

    def nrm(k, shape, scale):
        return scale * jax.random.normal(k, shape, f32)

    def gain(k, shape):
        return 1.0 + 0.05 * jax.random.normal(k, shape, f32)

    cl = [min(w, PAST_LEN) for (w, _) in GROUPS]
    return {
        'x_prompt': nrm(ks[0], (BATCH, SEQ, D_MODEL), 1.0),
        'x_sample': nrm(ks[1], (DEC_BATCH, DEC_SEQ, D_MODEL), 1.0),
        'p_prompt': nrm(ks[2], (DEPTH, BATCH, SEQ, D_PLE), 1.0),
        'p_sample': nrm(ks[3], (DEPTH, DEC_BATCH, DEC_SEQ, D_PLE), 1.0),
        'cache_kv_w128': nrm(ks[4], (DEPTH, DEC_BATCH, cl[0], 2, HPG, HEAD_DIM), 1.0),
        'cache_kv_w512': nrm(ks[5], (DEPTH, DEC_BATCH, cl[1], 2, HPG, HEAD_DIM), 1.0),
        'cache_kv_w2048': nrm(ks[6], (DEPTH, DEC_BATCH, cl[2], 2, HPG, HEAD_DIM), 1.0),
        'state_conv': nrm(ks[7], (DEPTH, DEC_BATCH, CONV_K - 1, D_CONV), 0.5),
        'rel_bias': nrm(ks[8], (N_BUCKETS, N_HEADS), 0.5),
        'norm_mix_pre': gain(ks[9], (DEPTH, D_MODEL)),
        'w_in': nrm(ks[10], (DEPTH, D_MODEL, IN_W), D_MODEL ** -0.5),
        'conv_w': nrm(ks[11], (DEPTH, CONV_K, D_CONV), CONV_K ** -0.5),
        'conv_b': nrm(ks[12], (DEPTH, D_CONV), 0.02),
        'conv_ln_g': gain(ks[13], (DEPTH, D_CONV)),
        'conv_ln_b': nrm(ks[14], (DEPTH, D_CONV), 0.02),
        'w_conv_out': nrm(ks[15], (DEPTH, D_CONV, D_MODEL), D_CONV ** -0.5),
        'w_attn_out': nrm(ks[16], (DEPTH, ATTN_OUT_W, D_MODEL), ATTN_OUT_W ** -0.5),
        'w_out': nrm(ks[17], (DEPTH, D_MODEL, D_MODEL), D_MODEL ** -0.5),
        'norm_mix_post': gain(ks[18], (DEPTH, D_MODEL)),
        'norm_ffn_pre': gain(ks[19], (DEPTH, D_MODEL)),
        'w_ffn_in': nrm(ks[20], (DEPTH, D_MODEL, 2 * D_FF), D_MODEL ** -0.5),
        'w_ffn_out': nrm(ks[21], (DEPTH, D_FF, D_MODEL), D_FF ** -0.5),
        'norm_ffn_post': gain(ks[22], (DEPTH, D_MODEL)),
        'ple_norm': gain(ks[23], (DEPTH, D_MODEL)),
        'w_ple_gate': nrm(ks[24], (DEPTH, D_MODEL, D_MODEL), D_MODEL ** -0.5),
        'w_ple_proj': nrm(ks[25], (DEPTH, D_PLE, D_MODEL), D_PLE ** -0.5),
    }


def reference(x_prompt, x_sample, p_prompt, p_sample, cache_kv_w128, cache_kv_w512, cache_kv_w2048,
              state_conv, rel_bias, norm_mix_pre, w_in, conv_w, conv_b, conv_ln_g, conv_ln_b, w_conv_out,
              w_attn_out, w_out, norm_mix_post, norm_ffn_pre, w_ffn_in, w_ffn_out, norm_ffn_post,
              ple_norm, w_ple_gate, w_ple_proj):
    caches = (cache_kv_w128, cache_kv_w512, cache_kv_w2048)
    yp, ys = x_prompt, x_sample
    kvp = [[] for _ in GROUPS]
    kvs = [[] for _ in GROUPS]
    convp, convs = [], []
    for i in range(DEPTH):
        lw = (norm_mix_pre[i], w_in[i], conv_w[i], conv_b[i], conv_ln_g[i], conv_ln_b[i], w_conv_out[i],
              w_attn_out[i], w_out[i], norm_mix_post[i], norm_ffn_pre[i], w_ffn_in[i], w_ffn_out[i],
              norm_ffn_post[i], ple_norm[i], w_ple_gate[i], w_ple_proj[i])
        prompt_fn = functools.partial(prompt_attention, rel_bias=rel_bias)
        sample_fn = functools.partial(sample_attention, rel_bias=rel_bias, kv_caches=[c[i] for c in caches])
        zero_hist = jnp.zeros((yp.shape[0], CONV_K - 1, D_CONV), yp.dtype)
        yp, bp, cp = decoder_layer(yp, p_prompt[i], prompt_fn, zero_hist, *lw)
        ys, bs, cs = decoder_layer(ys, p_sample[i], sample_fn, state_conv[i], *lw)
        for g in range(len(GROUPS)):
            kvp[g].append(bp[g])
            kvs[g].append(bs[g])
        convp.append(cp)
        convs.append(cs)
    kv128_p = jnp.stack(kvp[0], axis=0)
    kv512_p = jnp.stack(kvp[1], axis=0)
    kv2048_p = jnp.stack(kvp[2], axis=0)
    conv_p = jnp.stack(convp, axis=0)
    kv128_s = jnp.stack(kvs[0], axis=0)
    kv512_s = jnp.stack(kvs[1], axis=0)
    kv2048_s = jnp.stack(kvs[2], axis=0)
    conv_s = jnp.stack(convs, axis=0)
    return (yp, ys, kv128_p, kv512_p, kv2048_p, conv_p, kv128_s, kv512_s, kv2048_s, conv_s)
```

```python
import functools

import numpy as np
import jax
import jax.numpy as jnp
from jax import lax
from jax.experimental import pallas as pl
from jax.experimental.pallas import tpu as pltpu

F32 = jnp.float32
BF16 = jnp.bfloat16

D_MODEL = 1024
HEAD_DIM = 64
HPG = 4
GROUPS = ((128, 1), (512, 4), (2048, 16))
N_GROUPS = len(GROUPS)
GW = HPG * HEAD_DIM
ATTN_W = N_GROUPS * GW
D_CONV = D_MODEL
CONV_K = 31
D_FF = 2816
D_PLE = 256
N_BUCKETS = 32
MAX_DIST = 2048
BLK = 128
N_BACK = 128
EPS = 1e-6
NEG = -1e30
SCALE = HEAD_DIM ** -0.5
U0 = 3 * ATTN_W
G0 = U0 + 2 * D_CONV
IN_W = G0 + 2 * D_MODEL

ATT_TILE = 2048
CONV_HALO = 32
SUBLANES = 8
VMEM_LIMIT = 58 * 1024 * 1024


def _t5_bucket(dist):
    dist = np.asarray(dist).astype(np.int32)
    max_exact = N_BUCKETS // 2
    d = np.maximum(dist, 1).astype(np.float32)
    large = max_exact + np.floor(np.log(d / max_exact) / np.log(MAX_DIST / max_exact)
                                 * (N_BUCKETS - max_exact)).astype(np.int32)
    large = np.minimum(large, N_BUCKETS - 1)
    return np.where(dist < max_exact, dist, large).astype(np.int32)


def _rms(x, g):
    return x * lax.rsqrt(jnp.mean(x * x, axis=-1, keepdims=True) + EPS) * g


def _const_spec(shape):
    nd = len(shape)
    return pl.BlockSpec(shape, lambda *_: (0,) * nd, pipeline_mode=pl.Buffered(1))


def _params(sem):
    return pltpu.CompilerParams(dimension_semantics=sem, vmem_limit_bytes=VMEM_LIMIT)


def _inproj_body(x_ref, g_ref, w_ref, q0, q1, q2, k0, k1, k2, v0, v1, v2, z_ref, gate_ref):
    h = _rms(x_ref[...], g_ref[...]).astype(BF16)

    def proj(c0, c1):
        return jnp.dot(h, w_ref[:, c0:c1], preferred_element_type=F32)

    q = proj(0, ATTN_W) * SCALE
    k = proj(ATTN_W, 2 * ATTN_W)
    v = proj(2 * ATTN_W, 3 * ATTN_W)
    for g, (qo, ko, vo) in enumerate(((q0, k0, v0), (q1, k1, v1), (q2, k2, v2))):
        qo[...] = q[:, g * GW:(g + 1) * GW]
        ko[...] = k[:, g * GW:(g + 1) * GW]
        vo[...] = v[:, g * GW:(g + 1) * GW]
    a = proj(U0, U0 + D_CONV)
    gte = proj(U0 + D_CONV, G0)
    z_ref[...] = a * jax.nn.sigmoid(gte)
    gate_ref[...] = jax.nn.sigmoid(proj(G0, IN_W))


def _inproj(x2d, norm_g, w_in_bf16, tm):
    n = x2d.shape[0]
    row = lambda w: pl.BlockSpec((tm, w), lambda i: (i, 0))
    out_shape = ([jax.ShapeDtypeStruct((n, GW), F32)] * 9
                 + [jax.ShapeDtypeStruct((n, D_CONV), F32), jax.ShapeDtypeStruct((n, 2 * D_MODEL), F32)])
    return pl.pallas_call(
        _inproj_body,
        grid=(n // tm,),
        in_specs=[row(D_MODEL), _const_spec((1, D_MODEL)), _const_spec((D_MODEL, IN_W))],
        out_specs=[row(GW)] * 9 + [row(D_CONV), row(2 * D_MODEL)],
        out_shape=out_shape,
        compiler_params=_params(("parallel",)),
        name="inproj",
    )(x2d, norm_g, w_in_bf16)


def _attn_body(q_ref, kc_ref, kp_ref, vc_ref, vp_ref, bias_ref, o_ref, lse_ref, kbuf, vbuf, *, dil, nrb):
    tile = pl.program_id(1)
    kbuf[0:BLK, :] = kp_ref[0].astype(BF16)
    kbuf[BLK:, :] = kc_ref[0].astype(BF16)
    vbuf[0:BLK, :] = vp_ref[0].astype(BF16)
    vbuf[BLK:, :] = vc_ref[0].astype(BF16)
    head = lax.broadcasted_iota(jnp.int32, (1, GW), 1) // HEAD_DIM
    col = lax.broadcasted_iota(jnp.int32, (1, 2 * BLK), 1)

    def row_block(rb, carry):
        r0 = pl.multiple_of(rb * BLK, BLK)
        dead_cols = jnp.where(jnp.logical_and(tile == 0, rb == 0), BLK, 0)
        for r in range(dil):
            cs = slice(r * GW, (r + 1) * GW)
            q = q_ref[0, pl.ds(r0, BLK), cs]
            qs = jnp.concatenate([jnp.where(head == h, q, 0.0) for h in range(HPG)], axis=0).astype(BF16)
            kk = kbuf[pl.ds(r0, 2 * BLK), cs]
            vv = vbuf[pl.ds(r0, 2 * BLK), cs]
            s = lax.dot_general(qs, kk, (((1,), (1,)), ((), ())), preferred_element_type=F32)
            s = s + bias_ref[...]
            s = jnp.where(col < dead_cols, NEG, s)
            m = jnp.max(s, axis=-1, keepdims=True)
            p = jnp.exp(s - m)
            l = jnp.sum(p, axis=-1, keepdims=True)
            pv = jnp.dot(p.astype(BF16), vv, preferred_element_type=F32)
            on = pv / l
            lse = m + jnp.log(l)
            o = jnp.zeros((BLK, GW), F32)
            le = jnp.zeros((BLK, GW), F32)
            for h in range(HPG):
                hs = slice(h * BLK, (h + 1) * BLK)
                o = jnp.where(head == h, on[hs], o)
                le = jnp.where(head == h, lse[hs], le)
            o_ref[0, pl.ds(r0, BLK), cs] = o
            lse_ref[0, pl.ds(r0, BLK), cs] = le
        return carry

    lax.fori_loop(0, nrb, row_block, 0)


def _prompt_attention(q, k, v, bias, dil):
    b, rows_total, w = q.shape
    rows = ATT_TILE // dil
    nrb = rows // BLK
    cur = pl.BlockSpec((1, rows, w), lambda bi, i: (bi, i, 0))
    prev = pl.BlockSpec((1, BLK, w), lambda bi, i: (bi, jnp.maximum(i * nrb - 1, 0), 0))
    return pl.pallas_call(
        functools.partial(_attn_body, dil=dil, nrb=nrb),
        grid=(b, rows_total // rows),
        in_specs=[cur, cur, prev, cur, prev, _const_spec((HPG * BLK, 2 * BLK))],
        out_specs=[cur, cur],
        out_shape=[jax.ShapeDtypeStruct(q.shape, F32)] * 2,
        scratch_shapes=[pltpu.VMEM((rows + BLK, w), BF16)] * 2,
        compiler_params=_params(("parallel", "arbitrary")),
        name=f"prompt_attn_d{dil}",
    )(q, k, k, v, v, bias)


def _ln_swish(y, g, b):
    mu = jnp.mean(y, axis=-1, keepdims=True)
    yc = y - mu
    yn = yc * lax.rsqrt(jnp.mean(yc * yc, axis=-1, keepdims=True) + EPS) * g + b
    return yn * jax.nn.sigmoid(yn)


def _conv_body(z_ref, zp_ref, w_ref, cb_ref, lg_ref, lb_ref, o_ref, zc, yb, *, tm, rc):
    tile = pl.program_id(1)
    zc[0:CONV_HALO, :] = jnp.where(tile == 0, 0.0, zp_ref[0])
    zc[CONV_HALO:, :] = z_ref[0]
    lead = CONV_HALO - (CONV_K - 1)

    def rows(i, carry):
        r0 = pl.multiple_of(i * rc, rc)
        for c in range(D_CONV // 128):
            ls = slice(c * 128, (c + 1) * 128)
            win = zc[pl.ds(r0, rc + CONV_HALO), ls]
            acc = jnp.zeros((rc, 128), F32) + cb_ref[:, ls]
            for t in range(CONV_K):
                acc = acc + w_ref[t:t + 1, ls] * win[lead + t:lead + t + rc]
            yb[pl.ds(r0, rc), ls] = acc
        return carry

    lax.fori_loop(0, tm // rc, rows, 0)
    o_ref[0] = _ln_swish(yb[...], lg_ref[...], lb_ref[...])


def _prompt_conv(z, conv_w, conv_b, ln_g, ln_b, tm=512, rc=64):
    b, s, _ = z.shape
    cur = pl.BlockSpec((1, tm, D_CONV), lambda bi, i: (bi, i, 0))
    per = tm // CONV_HALO
    prev = pl.BlockSpec((1, CONV_HALO, D_CONV), lambda bi, i: (bi, jnp.maximum(i * per - 1, 0), 0))
    vec = _const_spec((1, D_CONV))
    return pl.pallas_call(
        functools.partial(_conv_body, tm=tm, rc=rc),
        grid=(b, s // tm),
        in_specs=[cur, prev, _const_spec((CONV_K, D_CONV)), vec, vec, vec],
        out_specs=cur,
        out_shape=jax.ShapeDtypeStruct(z.shape, F32),
        scratch_shapes=[pltpu.VMEM((tm + CONV_HALO, D_CONV), F32), pltpu.VMEM((tm, D_CONV), F32)],
        compiler_params=_params(("parallel", "arbitrary")),
        name="prompt_conv",
    )(z, z, conv_w, conv_b, ln_g, ln_b)


def _sconv_body(st_ref, z_ref, w_ref, cb_ref, lg_ref, lb_ref, o_ref, st_out_ref):
    z = z_ref[...]
    y = z * w_ref[CONV_K - 1:CONV_K, :] + cb_ref[...]
    for t in range(CONV_K - 1):
        y = y + st_ref[t] * w_ref[t:t + 1, :]
    o_ref[...] = _ln_swish(y, lg_ref[...], lb_ref[...])
    st_out_ref[0:CONV_K - 2] = st_ref[1:CONV_K - 1]
    st_out_ref[CONV_K - 2] = z


def _sample_conv(state_t, z, conv_w, conv_b, ln_g, ln_b, bt=32):
    hist, bd, _ = state_t.shape
    st_spec = pl.BlockSpec((hist, bt, D_CONV), lambda i: (0, i, 0))
    row = pl.BlockSpec((bt, D_CONV), lambda i: (i, 0))
    vec = _const_spec((1, D_CONV))
    return pl.pallas_call(
        _sconv_body,
        grid=(bd // bt,),
        in_specs=[st_spec, row, _const_spec((CONV_K, D_CONV)), vec, vec, vec],
        out_specs=[row, st_spec],
        out_shape=[jax.ShapeDtypeStruct((bd, D_CONV), F32), jax.ShapeDtypeStruct(state_t.shape, F32)],
        compiler_params=_params(("parallel",)),
        name="sample_conv",
    )(state_t, z, conv_w, conv_b, ln_g, ln_b)


def _sattn_body(q_ref, kn_ref, vn_ref, knt_ref, vnt_ref, c0_ref, c1_ref, c2_ref, b0_ref, b1_ref, b2_ref, bnew_ref,
                o_ref, lse_ref, n0_ref, n1_ref, n2_ref):
    b = pl.program_id(0)
    row_head = lax.broadcasted_iota(jnp.int32, (SUBLANES, GW), 0)
    lane_head = lax.broadcasted_iota(jnp.int32, (SUBLANES, GW), 1) // HEAD_DIM
    own = row_head == lane_head
    seq_lane = lax.broadcasted_iota(jnp.int32, (1, knt_ref.shape[1]), 1)
    groups = ((c0_ref, b0_ref, n0_ref), (c1_ref, b1_ref, n1_ref), (c2_ref, b2_ref, n2_ref))
    for g, (c_ref, bias_ref, new_ref) in enumerate(groups):
        gs = slice(g * GW, (g + 1) * GW)
        length = c_ref.shape[-1]
        kt = c_ref[0, 0]
        vt = c_ref[0, 1]
        q_rows = jnp.where(own, q_ref[0, :, gs], 0.0)
        s = jnp.dot(q_rows.astype(BF16), kt.astype(BF16), preferred_element_type=F32) + bias_ref[...]
        s_new = jnp.sum(q_rows * kn_ref[0, :, gs], axis=-1, keepdims=True) + bnew_ref[g]
        m = jnp.maximum(jnp.max(s, axis=-1, keepdims=True), s_new)
        p = jnp.exp(s - m)
        p_new = jnp.exp(s_new - m)
        l = jnp.sum(p, axis=-1, keepdims=True) + p_new
        pv = lax.dot_general(p.astype(BF16), vt.astype(BF16), (((1,), (1,)), ((), ())),
                             preferred_element_type=F32)
        on = (pv + p_new * vn_ref[0, :, gs]) / l
        lse = m + jnp.log(l)
        o_ref[0, :, gs] = jnp.sum(jnp.where(own, on, 0.0), axis=0, keepdims=True)
        lse_ref[0, :, gs] = jnp.sum(jnp.where(own, lse, 0.0), axis=0, keepdims=True)
        pos = lax.broadcasted_iota(jnp.int32, (1, length), 1)
        for kv, newt_ref in enumerate((knt_ref, vnt_ref)):
            new_col = jnp.sum(jnp.where(seq_lane == b, newt_ref[gs, :], 0.0), axis=-1, keepdims=True)
            rolled = pltpu.roll(c_ref[0, kv], length - 1, 1)
            new_ref[0, kv] = jnp.where(pos == length - 1, new_col, rolled)


def _sample_attention(q, kn, vn, knt, vnt, caches_t, biases, bias_new):
    bd = q.shape[0]
    row = pl.BlockSpec((1, 1, ATTN_W), lambda i: (i, 0, 0))
    cache_specs = [pl.BlockSpec((1,) + c.shape[1:], lambda i: (i, 0, 0, 0)) for c in caches_t]
    return pl.pallas_call(
        _sattn_body,
        grid=(bd,),
        in_specs=([row, row, row, _const_spec(knt.shape), _const_spec(vnt.shape)] + cache_specs
                  + [_const_spec(bi.shape) for bi in biases] + [_const_spec(bias_new.shape)]),
        out_specs=[row, row] + cache_specs,
        out_shape=([jax.ShapeDtypeStruct(q.shape, F32)] * 2
                   + [jax.ShapeDtypeStruct(c.shape, F32) for c in caches_t]),
        compiler_params=_params(("arbitrary",)),
        name="sample_attn",
    )(q, kn, vn, knt, vnt, *caches_t, *biases, bias_new)


def _post_body(x_ref, o0, o1, o2, l0, l1, l2, c_ref, g_ref, p_ref,
               wa_ref, wc_ref, wo_ref, n1_ref, n2_ref, wfi_ref, wfo_ref, n3_ref, n4_ref, wpg_ref, wpp_ref, y_ref):
    def mm(act, w_ref):
        return jnp.dot(act.astype(BF16), w_ref[...], preferred_element_type=F32)

    lses = (l0[...], l1[...], l2[...])
    mx = jnp.maximum(jnp.maximum(lses[0], lses[1]), lses[2])
    es = [jnp.exp(l - mx) for l in lses]
    o = (es[0] * o0[...] + es[1] * o1[...] + es[2] * o2[...]) / (es[0] + es[1] + es[2])
    a = mm(o, wa_ref)
    c = mm(c_ref[...], wc_ref)
    mix = mm(g_ref[:, 0:D_MODEL] * a + g_ref[:, D_MODEL:] * c, wo_ref)
    x = x_ref[...] + _rms(mix, n1_ref[...])
    h2 = _rms(x, n2_ref[...]).astype(BF16)
    gg = jnp.dot(h2, wfi_ref[:, 0:D_FF], preferred_element_type=F32)
    uu = jnp.dot(h2, wfi_ref[:, D_FF:], preferred_element_type=F32)
    f = mm(gg * jax.nn.sigmoid(gg) * uu, wfo_ref)
    x = x + _rms(f, n3_ref[...])
    gate = jax.nn.sigmoid(mm(_rms(x, n4_ref[...]), wpg_ref))
    y_ref[...] = x + gate * mm(p_ref[...], wpp_ref)


def _post(x2d, os_, lses, c_act, gates, p2d, weights, tm):
    n = x2d.shape[0]
    row = lambda w: pl.BlockSpec((tm, w), lambda i: (i, 0))
    w_specs = [_const_spec(w.shape) for w in weights]
    return pl.pallas_call(
        _post_body,
        grid=(n // tm,),
        in_specs=[row(D_MODEL)] + [row(GW)] * 6 + [row(D_CONV), row(2 * D_MODEL), row(D_PLE)] + w_specs,
        out_specs=row(D_MODEL),
        out_shape=jax.ShapeDtypeStruct((n, D_MODEL), F32),
        compiler_params=_params(("parallel",)),
        name="post",
    )(x2d, *os_, *lses, c_act, gates, p2d, *weights)


def _prompt_bias(rel_bias, g, dil):
    qq = np.arange(BLK)[:, None]
    kk = np.arange(2 * BLK)[None, :]
    j = qq + BLK - kk
    band = (j >= 0) & (j <= N_BACK)
    bucket = _t5_bucket(np.clip(j, 0, N_BACK) * dil)
    tab = jnp.transpose(rel_bias[:, g * HPG:(g + 1) * HPG][bucket], (2, 0, 1)).astype(F32)
    return jnp.where(band[None], tab, NEG).reshape(HPG * BLK, 2 * BLK)


def _sample_bias(rel_bias, g, dil, length):
    dist = length - np.arange(length)
    tab = rel_bias[:, g * HPG:(g + 1) * HPG]
    old = jnp.where((dist % dil == 0)[None], jnp.transpose(tab[_t5_bucket(dist)]), NEG)
    new = tab[_t5_bucket(np.zeros(1))].reshape(HPG, 1)
    pad = lambda a: jnp.concatenate([a, jnp.zeros((SUBLANES - HPG, a.shape[1]), F32)], axis=0)
    return pad(old.astype(F32)), pad(new.astype(F32))


def _layer(x_prompt, x_sample, p_prompt, p_sample, caches, state_conv, rel_bias, lw):
    (norm_mix_pre, w_in, conv_w, conv_b, conv_ln_g, conv_ln_b, w_conv_out, w_attn_out, w_out, norm_mix_post,
     norm_ffn_pre, w_ffn_in, w_ffn_out, norm_ffn_post, ple_norm, w_ple_gate, w_ple_proj) = lw
    b, s, _ = x_prompt.shape
    bd = x_sample.shape[0]
    vec = lambda a: a.reshape(1, -1)
    w_in_b = w_in.astype(BF16)
    post_w = (w_attn_out.astype(BF16), w_conv_out.astype(BF16), w_out.astype(BF16), vec(norm_mix_post),
              vec(norm_ffn_pre), w_ffn_in.astype(BF16), w_ffn_out.astype(BF16), vec(norm_ffn_post),
              vec(ple_norm), w_ple_gate.astype(BF16), w_ple_proj.astype(BF16))

    xp = x_prompt.reshape(b * s, D_MODEL)
    outs = _inproj(xp, vec(norm_mix_pre), w_in_b, tm=512)
    qkv, z_p, gates_p = outs[:9], outs[9], outs[10]
    os_, lses, kv_p = [], [], []
    for g, (win, dil) in enumerate(GROUPS):
        q, k, v = (t.reshape(b, s // dil, dil * GW) for t in qkv[g::N_GROUPS])
        o, lse = _prompt_attention(q, k, v, _prompt_bias(rel_bias, g, dil), dil)
        os_.append(o.reshape(b * s, GW))
        lses.append(lse.reshape(b * s, GW))
        keep = min(win, s)
        kt = qkv[N_GROUPS + g].reshape(b, s, HPG, HEAD_DIM)[:, s - keep:]
        vt = qkv[2 * N_GROUPS + g].reshape(b, s, HPG, HEAD_DIM)[:, s - keep:]
        kv_p.append(jnp.stack([kt, vt], axis=2))
    z_p = z_p.reshape(b, s, D_CONV)
    c_p = _prompt_conv(z_p, conv_w, vec(conv_b), vec(conv_ln_g), vec(conv_ln_b))
    y_p = _post(xp, os_, lses, c_p.reshape(b * s, D_CONV), gates_p, p_prompt.reshape(b * s, D_PLE), post_w, tm=256)
    conv_tail_p = z_p[:, s - (CONV_K - 1):]

    xs = x_sample.reshape(bd, D_MODEL)
    outs = _inproj(xs, vec(norm_mix_pre), w_in_b, tm=bd)
    qkv, z_s, gates_s = outs[:9], outs[9], outs[10]
    q_s, k_s, v_s = (jnp.concatenate(qkv[N_GROUPS * i:N_GROUPS * (i + 1)], axis=1) for i in range(3))
    caches_t, biases, bias_new = [], [], []
    for g, (win, dil) in enumerate(GROUPS):
        length = caches[g].shape[1]
        caches_t.append(jnp.transpose(caches[g], (0, 2, 3, 4, 1)).reshape(bd, 2, GW, length))
        old, new = _sample_bias(rel_bias, g, dil, length)
        biases.append(old)
        bias_new.append(new)
    r3 = lambda t: t.reshape(bd, 1, ATTN_W)
    sa = _sample_attention(r3(q_s), r3(k_s), r3(v_s), k_s.T, v_s.T, caches_t, biases, jnp.stack(bias_new))
    o_s, lse_s, new_caches = sa[0].reshape(bd, ATTN_W), sa[1].reshape(bd, ATTN_W), sa[2:]
    c_s, conv_s = _sample_conv(jnp.transpose(state_conv, (1, 0, 2)), z_s, conv_w, vec(conv_b),
                               vec(conv_ln_g), vec(conv_ln_b))
    y_s = _post(xs, [o_s[:, g * GW:(g + 1) * GW] for g in range(N_GROUPS)],
                [lse_s[:, g * GW:(g + 1) * GW] for g in range(N_GROUPS)],
                c_s, gates_s, p_sample.reshape(bd, D_PLE), post_w, tm=bd)
    kv_s = [jnp.transpose(c.reshape(bd, 2, HPG, HEAD_DIM, c.shape[-1]), (0, 4, 1, 2, 3)) for c in new_caches]
    return (y_p.reshape(b, s, D_MODEL), y_s.reshape(bd, 1, D_MODEL), kv_p, conv_tail_p, kv_s,
            jnp.transpose(conv_s, (1, 0, 2)))


def kernel(x_prompt, x_sample, p_prompt, p_sample, cache_kv_w128, cache_kv_w512, cache_kv_w2048, state_conv,
           rel_bias, norm_mix_pre, w_in, conv_w, conv_b, conv_ln_g, conv_ln_b, w_conv_out, w_attn_out, w_out,
           norm_mix_post, norm_ffn_pre, w_ffn_in, w_ffn_out, norm_ffn_post, ple_norm, w_ple_gate, w_ple_proj):
    depth = w_in.shape[0]
    assert depth == 1, "cache handling below is written for a single layer"
    caches = (cache_kv_w128[0], cache_kv_w512[0], cache_kv_w2048[0])
    lw = tuple(t[0] for t in (norm_mix_pre, w_in, conv_w, conv_b, conv_ln_g, conv_ln_b, w_conv_out, w_attn_out,
                              w_out, norm_mix_post, norm_ffn_pre, w_ffn_in, w_ffn_out, norm_ffn_post, ple_norm,
                              w_ple_gate, w_ple_proj))
    y_p, y_s, kv_p, conv_p, kv_s, conv_s = _layer(x_prompt, x_sample, p_prompt[0], p_sample[0], caches,
                                                  state_conv[0], rel_bias, lw)
    return (y_p, y_s, kv_p[0][None], kv_p[1][None], kv_p[2][None], conv_p[None],
            kv_s[0][None], kv_s[1][None], kv_s[2][None], conv_s[None])
```

```python
import functools

import numpy as np
import jax
import jax.numpy as jnp
from jax import lax
from jax.experimental import pallas as pl
from jax.experimental.pallas import tpu as pltpu

F32 = jnp.float32
BF16 = jnp.bfloat16

D_MODEL = 1024
HEAD_DIM = 64
HPG = 4
GROUPS = ((128, 1), (512, 4), (2048, 16))
N_GROUPS = len(GROUPS)
GW = HPG * HEAD_DIM
ATTN_W = N_GROUPS * GW
D_CONV = D_MODEL
CONV_K = 31
D_FF = 2816
D_PLE = 256
N_BUCKETS = 32
MAX_DIST = 2048
BLK = 128
N_BACK = 128
EPS = 1e-6
NEG = -1e30
SCALE = HEAD_DIM ** -0.5
U0 = 3 * ATTN_W
G0 = U0 + 2 * D_CONV
IN_W = G0 + 2 * D_MODEL

ATT_TILE = 2048
CONV_HALO = 32
SUBLANES = 8
LANES = 128
SLABS = GW // LANES
VMEM_LIMIT = 58 * 1024 * 1024


def _t5_bucket(dist):
    dist = np.asarray(dist).astype(np.int32)
    max_exact = N_BUCKETS // 2
    d = np.maximum(dist, 1).astype(np.float32)
    large = max_exact + np.floor(np.log(d / max_exact) / np.log(MAX_DIST / max_exact)
                                 * (N_BUCKETS - max_exact)).astype(np.int32)
    large = np.minimum(large, N_BUCKETS - 1)
    return np.where(dist < max_exact, dist, large).astype(np.int32)


def _rms(x, g):
    return x * lax.rsqrt(jnp.mean(x * x, axis=-1, keepdims=True) + EPS) * g


def _const_spec(shape):
    nd = len(shape)
    return pl.BlockSpec(shape, lambda *_: (0,) * nd, pipeline_mode=pl.Buffered(1))


def _params(sem):
    return pltpu.CompilerParams(dimension_semantics=sem, vmem_limit_bytes=VMEM_LIMIT)


def _inproj_body(x_ref, g_ref, w_ref, q0, q1, q2, k0, k1, k2, v0, v1, v2, z_ref, gate_ref):
    h = _rms(x_ref[...], g_ref[...]).astype(BF16)

    def proj(c0, c1):
        return jnp.dot(h, w_ref[:, c0:c1], preferred_element_type=F32)

    q = proj(0, ATTN_W) * SCALE
    k = proj(ATTN_W, 2 * ATTN_W)
    v = proj(2 * ATTN_W, 3 * ATTN_W)
    for g, (qo, ko, vo) in enumerate(((q0, k0, v0), (q1, k1, v1), (q2, k2, v2))):
        for sl in range(SLABS):
            cs = slice(g * GW + sl * LANES, g * GW + (sl + 1) * LANES)
            qo[sl] = q[:, cs]
            ko[sl] = k[:, cs]
            vo[sl] = v[:, cs]
    a = proj(U0, U0 + D_CONV)
    gte = proj(U0 + D_CONV, G0)
    z_ref[...] = a * jax.nn.sigmoid(gte)
    gate_ref[...] = jax.nn.sigmoid(proj(G0, IN_W))


def _inproj(x2d, norm_g, w_in_bf16, tm):
    n = x2d.shape[0]
    row = lambda w: pl.BlockSpec((tm, w), lambda i: (i, 0))
    slab = pl.BlockSpec((SLABS, tm, LANES), lambda i: (0, i, 0))
    out_shape = ([jax.ShapeDtypeStruct((SLABS, n, LANES), F32)] * 9
                 + [jax.ShapeDtypeStruct((n, D_CONV), F32), jax.ShapeDtypeStruct((n, 2 * D_MODEL), F32)])
    return pl.pallas_call(
        _inproj_body,
        grid=(n // tm,),
        in_specs=[row(D_MODEL), _const_spec((1, D_MODEL)), _const_spec((D_MODEL, IN_W))],
        out_specs=[slab] * 9 + [row(D_CONV), row(2 * D_MODEL)],
        out_shape=out_shape,
        compiler_params=_params(("parallel",)),
        name="inproj",
    )(x2d, norm_g, w_in_bf16)


def _attn_body(q_ref, kc_ref, kp_ref, vc_ref, vp_ref, bias_ref, o_ref, lse_ref, *, dil, nrb):
    tile = pl.program_id(1)
    head = lax.broadcasted_iota(jnp.int32, (1, GW), 1) // HEAD_DIM
    col = lax.broadcasted_iota(jnp.int32, (1, 2 * BLK), 1)

    def sub_rows(start, n):
        return pl.ds(start, n, stride=dil) if dil > 1 else pl.ds(start, n)

    def load(ref, start, n):
        return jnp.concatenate([ref[sl, sub_rows(start, n), :] for sl in range(SLABS)], axis=1)

    def block(r0, kk, vv, dead_cols):
        q = load(q_ref, r0, BLK)
        qs = jnp.concatenate([jnp.where(head == h, q, 0.0) for h in range(HPG)], axis=0).astype(BF16)
        s = lax.dot_general(qs, kk.astype(BF16), (((1,), (1,)), ((), ())), preferred_element_type=F32)
        s = s + bias_ref[...]
        if dead_cols is not None:
            s = jnp.where(col < dead_cols, NEG, s)
        m = jnp.max(s, axis=-1, keepdims=True)
        p = jnp.exp(s - m)
        l = jnp.sum(p, axis=-1, keepdims=True)
        pv = jnp.dot(p.astype(BF16), vv.astype(BF16), preferred_element_type=F32)
        on = pv / l
        lse = m + jnp.log(l)
        o = jnp.zeros((BLK, GW), F32)
        le = jnp.zeros((BLK, GW), F32)
        for h in range(HPG):
            hs = slice(h * BLK, (h + 1) * BLK)
            o = jnp.where(head == h, on[hs], o)
            le = jnp.where(head == h, lse[hs], le)
        for sl in range(SLABS):
            o_ref[sl, sub_rows(r0, BLK), :] = o[:, sl * LANES:(sl + 1) * LANES]
            lse_ref[sl, sub_rows(r0, BLK), :] = le[:, sl * LANES:(sl + 1) * LANES]

    dead_cols = jnp.where(tile == 0, BLK, 0)
    for r in range(dil):
        kk = jnp.concatenate([load(kp_ref, r, BLK), load(kc_ref, r, BLK)], axis=0)
        vv = jnp.concatenate([load(vp_ref, r, BLK), load(vc_ref, r, BLK)], axis=0)
        block(r, kk, vv, dead_cols)

    def later_block(rb, carry):
        base = pl.multiple_of((rb - 1) * (BLK * dil), BLK * dil)
        for r in range(dil):
            block(base + BLK * dil + r, load(kc_ref, base + r, 2 * BLK), load(vc_ref, base + r, 2 * BLK), None)
        return carry

    if nrb > 1:
        lax.fori_loop(1, nrb, later_block, 0)


def _prompt_attention(q, k, v, bias, dil, batch):
    n = q.shape[1]
    tiles = n // batch // ATT_TILE
    nrb = ATT_TILE // dil // BLK
    halo = BLK * dil
    per = ATT_TILE // halo
    cur = pl.BlockSpec((SLABS, ATT_TILE, LANES), lambda bi, i: (0, bi * tiles + i, 0))
    prev = pl.BlockSpec((SLABS, halo, LANES), lambda bi, i: (0, jnp.maximum((bi * tiles + i) * per - 1, 0), 0))
    return pl.pallas_call(
        functools.partial(_attn_body, dil=dil, nrb=nrb),
        grid=(batch, tiles),
        in_specs=[cur, cur, prev, cur, prev, _const_spec((HPG * BLK, 2 * BLK))],
        out_specs=[cur, cur],
        out_shape=[jax.ShapeDtypeStruct(q.shape, F32)] * 2,
        compiler_params=_params(("parallel", "arbitrary")),
        name=f"prompt_attn_d{dil}",
    )(q, k, k, v, v, bias)


def _ln_swish(y, g, b):
    mu = jnp.mean(y, axis=-1, keepdims=True)
    yc = y - mu
    yn = yc * lax.rsqrt(jnp.mean(yc * yc, axis=-1, keepdims=True) + EPS) * g + b
    return yn * jax.nn.sigmoid(yn)


def _conv_body(z_ref, zp_ref, w_ref, cb_ref, lg_ref, lb_ref, o_ref, zc, yb, *, tm, rc):
    tile = pl.program_id(1)
    zc[0:CONV_HALO, :] = jnp.where(tile == 0, 0.0, zp_ref[0])
    zc[CONV_HALO:, :] = z_ref[0]
    lead = CONV_HALO - (CONV_K - 1)

    def rows(i, carry):
        r0 = pl.multiple_of(i * rc, rc)
        for c in range(D_CONV // 128):
            ls = slice(c * 128, (c + 1) * 128)
            win = zc[pl.ds(r0, rc + CONV_HALO), ls]
            acc = jnp.zeros((rc, 128), F32) + cb_ref[:, ls]
            for t in range(CONV_K):
                acc = acc + w_ref[t:t + 1, ls] * win[lead + t:lead + t + rc]
            yb[pl.ds(r0, rc), ls] = acc
        return carry

    lax.fori_loop(0, tm // rc, rows, 0)
    o_ref[0] = _ln_swish(yb[...], lg_ref[...], lb_ref[...])


def _prompt_conv(z, conv_w, conv_b, ln_g, ln_b, tm=512, rc=64):
    b, s, _ = z.shape
    cur = pl.BlockSpec((1, tm, D_CONV), lambda bi, i: (bi, i, 0))
    per = tm // CONV_HALO
    prev = pl.BlockSpec((1, CONV_HALO, D_CONV), lambda bi, i: (bi, jnp.maximum(i * per - 1, 0), 0))
    vec = _const_spec((1, D_CONV))
    return pl.pallas_call(
        functools.partial(_conv_body, tm=tm, rc=rc),
        grid=(b, s // tm),
        in_specs=[cur, prev, _const_spec((CONV_K, D_CONV)), vec, vec, vec],
        out_specs=cur,
        out_shape=jax.ShapeDtypeStruct(z.shape, F32),
        scratch_shapes=[pltpu.VMEM((tm + CONV_HALO, D_CONV), F32), pltpu.VMEM((tm, D_CONV), F32)],
        compiler_params=_params(("parallel", "arbitrary")),
        name="prompt_conv",
    )(z, z, conv_w, conv_b, ln_g, ln_b)


def _sconv_body(st_ref, z_ref, w_ref, cb_ref, lg_ref, lb_ref, o_ref, st_out_ref):
    z = z_ref[...]
    y = z * w_ref[CONV_K - 1:CONV_K, :] + cb_ref[...]
    for t in range(CONV_K - 1):
        y = y + st_ref[t] * w_ref[t:t + 1, :]
    o_ref[...] = _ln_swish(y, lg_ref[...], lb_ref[...])
    st_out_ref[0:CONV_K - 2] = st_ref[1:CONV_K - 1]
    st_out_ref[CONV_K - 2] = z


def _sample_conv(state_t, z, conv_w, conv_b, ln_g, ln_b, bt=32):
    hist, bd, _ = state_t.shape
    st_spec = pl.BlockSpec((hist, bt, D_CONV), lambda i: (0, i, 0))
    row = pl.BlockSpec((bt, D_CONV), lambda i: (i, 0))
    vec = _const_spec((1, D_CONV))
    return pl.pallas_call(
        _sconv_body,
        grid=(bd // bt,),
        in_specs=[st_spec, row, _const_spec((CONV_K, D_CONV)), vec, vec, vec],
        out_specs=[row, st_spec],
        out_shape=[jax.ShapeDtypeStruct((bd, D_CONV), F32), jax.ShapeDtypeStruct(state_t.shape, F32)],
        compiler_params=_params(("parallel",)),
        name="sample_conv",
    )(state_t, z, conv_w, conv_b, ln_g, ln_b)


def _sattn_body(q_ref, kn_ref, vn_ref, knt_ref, vnt_ref, c0_ref, c1_ref, c2_ref, b0_ref, b1_ref, b2_ref, bnew_ref,
                o_ref, lse_ref, n0_ref, n1_ref, n2_ref):
    b = pl.program_id(0)
    row_head = lax.broadcasted_iota(jnp.int32, (SUBLANES, GW), 0)
    lane_head = lax.broadcasted_iota(jnp.int32, (SUBLANES, GW), 1) // HEAD_DIM
    own = row_head == lane_head
    seq_lane = lax.broadcasted_iota(jnp.int32, (1, knt_ref.shape[1]), 1)
    groups = ((c0_ref, b0_ref, n0_ref), (c1_ref, b1_ref, n1_ref), (c2_ref, b2_ref, n2_ref))
    for g, (c_ref, bias_ref, new_ref) in enumerate(groups):
        gs = slice(g * GW, (g + 1) * GW)
        length = c_ref.shape[-1]
        kt = c_ref[0, 0]
        vt = c_ref[0, 1]
        q_rows = jnp.where(own, q_ref[0, :, gs], 0.0)
        s = jnp.dot(q_rows.astype(BF16), kt.astype(BF16), preferred_element_type=F32) + bias_ref[...]
        s_new = jnp.sum(q_rows * kn_ref[0, :, gs], axis=-1, keepdims=True) + bnew_ref[g]
        m = jnp.maximum(jnp.max(s, axis=-1, keepdims=True), s_new)
        p = jnp.exp(s - m)
        p_new = jnp.exp(s_new - m)
        l = jnp.sum(p, axis=-1, keepdims=True) + p_new
        pv = lax.dot_general(p.astype(BF16), vt.astype(BF16), (((1,), (1,)), ((), ())),
                             preferred_element_type=F32)
        on = (pv + p_new * vn_ref[0, :, gs]) / l
        lse = m + jnp.log(l)
        o_ref[0, :, gs] = jnp.sum(jnp.where(own, on, 0.0), axis=0, keepdims=True)
        lse_ref[0, :, gs] = jnp.sum(jnp.where(own, lse, 0.0), axis=0, keepdims=True)
        pos = lax.broadcasted_iota(jnp.int32, (1, length), 1)
        for kv, newt_ref in enumerate((knt_ref, vnt_ref)):
            new_col = jnp.sum(jnp.where(seq_lane == b, newt_ref[gs, :], 0.0), axis=-1, keepdims=True)
            rolled = pltpu.roll(c_ref[0, kv], length - 1, 1)
            new_ref[0, kv] = jnp.where(pos == length - 1, new_col, rolled)


def _sample_attention(q, kn, vn, knt, vnt, caches_t, biases, bias_new):
    bd = q.shape[0]
    row = pl.BlockSpec((1, 1, ATTN_W), lambda i: (i, 0, 0))
    cache_specs = [pl.BlockSpec((1,) + c.shape[1:], lambda i: (i, 0, 0, 0)) for c in caches_t]
    return pl.pallas_call(
        _sattn_body,
        grid=(bd,),
        in_specs=([row, row, row, _const_spec(knt.shape), _const_spec(vnt.shape)] + cache_specs
                  + [_const_spec(bi.shape) for bi in biases] + [_const_spec(bias_new.shape)]),
        out_specs=[row, row] + cache_specs,
        out_shape=([jax.ShapeDtypeStruct(q.shape, F32)] * 2
                   + [jax.ShapeDtypeStruct(c.shape, F32) for c in caches_t]),
        compiler_params=_params(("arbitrary",)),
        name="sample_attn",
    )(q, kn, vn, knt, vnt, *caches_t, *biases, bias_new)


def _post_body(x_ref, o0, o1, o2, l0, l1, l2, c_ref, g_ref, p_ref,
               wa_ref, wc_ref, wo_ref, n1_ref, n2_ref, wfi_ref, wfo_ref, n3_ref, n4_ref, wpg_ref, wpp_ref, y_ref):
    def mm(act, w_ref):
        return jnp.dot(act.astype(BF16), w_ref[...], preferred_element_type=F32)

    wide = lambda ref: jnp.concatenate([ref[sl] for sl in range(SLABS)], axis=1)
    lses = (wide(l0), wide(l1), wide(l2))
    mx = jnp.maximum(jnp.maximum(lses[0], lses[1]), lses[2])
    es = [jnp.exp(l - mx) for l in lses]
    o = (es[0] * wide(o0) + es[1] * wide(o1) + es[2] * wide(o2)) / (es[0] + es[1] + es[2])
    a = mm(o, wa_ref)
    c = mm(c_ref[...], wc_ref)
    mix = mm(g_ref[:, 0:D_MODEL] * a + g_ref[:, D_MODEL:] * c, wo_ref)
    x = x_ref[...] + _rms(mix, n1_ref[...])
    h2 = _rms(x, n2_ref[...]).astype(BF16)
    gg = jnp.dot(h2, wfi_ref[:, 0:D_FF], preferred_element_type=F32)
    uu = jnp.dot(h2, wfi_ref[:, D_FF:], preferred_element_type=F32)
    f = mm(gg * jax.nn.sigmoid(gg) * uu, wfo_ref)
    x = x + _rms(f, n3_ref[...])
    gate = jax.nn.sigmoid(mm(_rms(x, n4_ref[...]), wpg_ref))
    y_ref[...] = x + gate * mm(p_ref[...], wpp_ref)


def _post(x2d, os_, lses, c_act, gates, p2d, weights, tm):
    n = x2d.shape[0]
    row = lambda w: pl.BlockSpec((tm, w), lambda i: (i, 0))
    w_specs = [_const_spec(w.shape) for w in weights]
    slab = pl.BlockSpec((SLABS, tm, LANES), lambda i: (0, i, 0))
    return pl.pallas_call(
        _post_body,
        grid=(n // tm,),
        in_specs=[row(D_MODEL)] + [slab] * 6 + [row(D_CONV), row(2 * D_MODEL), row(D_PLE)] + w_specs,
        out_specs=row(D_MODEL),
        out_shape=jax.ShapeDtypeStruct((n, D_MODEL), F32),
        compiler_params=_params(("parallel",)),
        name="post",
    )(x2d, *os_, *lses, c_act, gates, p2d, *weights)


def _bucket_lookup(tab, bucket):
    tab_t = jnp.transpose(tab).astype(F32)
    expand = (slice(None),) + (None,) * bucket.ndim
    out = jnp.zeros((tab_t.shape[0],) + bucket.shape, F32)
    for bkt in np.unique(bucket):
        out = jnp.where((bucket == bkt)[None], tab_t[:, int(bkt)][expand], out)
    return out


def _prompt_bias(rel_bias, g, dil):
    qq = np.arange(BLK)[:, None]
    kk = np.arange(2 * BLK)[None, :]
    j = qq + BLK - kk
    band = (j >= 0) & (j <= N_BACK)
    tab = _bucket_lookup(rel_bias[:, g * HPG:(g + 1) * HPG], _t5_bucket(np.clip(j, 0, N_BACK) * dil))
    return jnp.where(band[None], tab, NEG).reshape(HPG * BLK, 2 * BLK)


def _sample_bias(rel_bias, g, dil, length):
    dist = length - np.arange(length)
    tab = rel_bias[:, g * HPG:(g + 1) * HPG]
    old = jnp.where((dist % dil == 0)[None], _bucket_lookup(tab, _t5_bucket(dist)), NEG)
    new = _bucket_lookup(tab, _t5_bucket(np.zeros(1)))
    pad = lambda a: jnp.concatenate([a, jnp.zeros((SUBLANES - HPG, a.shape[1]), F32)], axis=0)
    return pad(old), pad(new)


def _layer(x_prompt, x_sample, p_prompt, p_sample, caches, state_conv, rel_bias, lw):
    (norm_mix_pre, w_in, conv_w, conv_b, conv_ln_g, conv_ln_b, w_conv_out, w_attn_out, w_out, norm_mix_post,
     norm_ffn_pre, w_ffn_in, w_ffn_out, norm_ffn_post, ple_norm, w_ple_gate, w_ple_proj) = lw
    b, s, _ = x_prompt.shape
    bd = x_sample.shape[0]
    vec = lambda a: a.reshape(1, -1)
    w_in_b = w_in.astype(BF16)
    post_w = (w_attn_out.astype(BF16), w_conv_out.astype(BF16), w_out.astype(BF16), vec(norm_mix_post),
              vec(norm_ffn_pre), w_ffn_in.astype(BF16), w_ffn_out.astype(BF16), vec(norm_ffn_post),
              vec(ple_norm), w_ple_gate.astype(BF16), w_ple_proj.astype(BF16))

    xp = x_prompt.reshape(b * s, D_MODEL)
    outs = _inproj(xp, vec(norm_mix_pre), w_in_b, tm=512)
    qkv, z_p, gates_p = outs[:9], outs[9], outs[10]
    unslab = lambda t: jnp.transpose(t, (1, 0, 2)).reshape(t.shape[1], GW)
    os_, lses, kv_p = [], [], []
    for g, (win, dil) in enumerate(GROUPS):
        q, k, v = qkv[g::N_GROUPS]
        o, lse = _prompt_attention(q, k, v, _prompt_bias(rel_bias, g, dil), dil, b)
        os_.append(o)
        lses.append(lse)
        keep = min(win, s)
        tail = lambda t: unslab(t.reshape(SLABS, b, s, LANES)[:, :, s - keep:].reshape(SLABS, b * keep, LANES))
        kv_p.append(jnp.stack([tail(k).reshape(b, keep, HPG, HEAD_DIM), tail(v).reshape(b, keep, HPG, HEAD_DIM)],
                              axis=2))
    z_p = z_p.reshape(b, s, D_CONV)
    c_p = _prompt_conv(z_p, conv_w, vec(conv_b), vec(conv_ln_g), vec(conv_ln_b))
    y_p = _post(xp, os_, lses, c_p.reshape(b * s, D_CONV), gates_p, p_prompt.reshape(b * s, D_PLE), post_w, tm=256)
    conv_tail_p = z_p[:, s - (CONV_K - 1):]

    xs = x_sample.reshape(bd, D_MODEL)
    outs = _inproj(xs, vec(norm_mix_pre), w_in_b, tm=bd)
    qkv, z_s, gates_s = outs[:9], outs[9], outs[10]
    q_s, k_s, v_s = (jnp.concatenate([unslab(t) for t in qkv[N_GROUPS * i:N_GROUPS * (i + 1)]], axis=1)
                     for i in range(3))
    caches_t, biases, bias_new = [], [], []
    for g, (win, dil) in enumerate(GROUPS):
        length = caches[g].shape[1]
        caches_t.append(jnp.transpose(caches[g], (0, 2, 3, 4, 1)).reshape(bd, 2, GW, length))
        old, new = _sample_bias(rel_bias, g, dil, length)
        biases.append(old)
        bias_new.append(new)
    r3 = lambda t: t.reshape(bd, 1, ATTN_W)
    sa = _sample_attention(r3(q_s), r3(k_s), r3(v_s), k_s.T, v_s.T, caches_t, biases, jnp.stack(bias_new))
    o_s, lse_s, new_caches = sa[0].reshape(bd, ATTN_W), sa[1].reshape(bd, ATTN_W), sa[2:]
    c_s, conv_s = _sample_conv(jnp.transpose(state_conv, (1, 0, 2)), z_s, conv_w, vec(conv_b),
                               vec(conv_ln_g), vec(conv_ln_b))
    slabs = lambda t, g: jnp.transpose(t[:, g * GW:(g + 1) * GW].reshape(bd, SLABS, LANES), (1, 0, 2))
    y_s = _post(xs, [slabs(o_s, g) for g in range(N_GROUPS)], [slabs(lse_s, g) for g in range(N_GROUPS)],
                c_s, gates_s, p_sample.reshape(bd, D_PLE), post_w, tm=bd)
    kv_s = [jnp.transpose(c.reshape(bd, 2, HPG, HEAD_DIM, c.shape[-1]), (0, 4, 1, 2, 3)) for c in new_caches]
    return (y_p.reshape(b, s, D_MODEL), y_s.reshape(bd, 1, D_MODEL), kv_p, conv_tail_p, kv_s,
            jnp.transpose(conv_s, (1, 0, 2)))


def kernel(x_prompt, x_sample, p_prompt, p_sample, cache_kv_w128, cache_kv_w512, cache_kv_w2048, state_conv,
           rel_bias, norm_mix_pre, w_in, conv_w, conv_b, conv_ln_g, conv_ln_b, w_conv_out, w_attn_out, w_out,
           norm_mix_post, norm_ffn_pre, w_ffn_in, w_ffn_out, norm_ffn_post, ple_norm, w_ple_gate, w_ple_proj):
    depth = w_in.shape[0]
    assert depth == 1, "cache handling below is written for a single layer"
    caches = (cache_kv_w128[0], cache_kv_w512[0], cache_kv_w2048[0])
    lw = tuple(t[0] for t in (norm_mix_pre, w_in, conv_w, conv_b, conv_ln_g, conv_ln_b, w_conv_out, w_attn_out,
                              w_out, norm_mix_post, norm_ffn_pre, w_ffn_in, w_ffn_out, norm_ffn_post, ple_norm,
                              w_ple_gate, w_ple_proj))
    y_p, y_s, kv_p, conv_p, kv_s, conv_s = _layer(x_prompt, x_sample, p_prompt[0], p_sample[0], caches,
                                                  state_conv[0], rel_bias, lw)
    return (y_p, y_s, kv_p[0][None], kv_p[1][None], kv_p[2][None], conv_p[None],
            kv_s[0][None], kv_s[1][None], kv_s[2][None], conv_s[None])
```
